```python
import math
import jax, jax.numpy as jnp
from jax import lax
import numpy as np

D_MODEL = 2048
BATCH = 8
SEQ = 2048
DEPTH = 4

N_MIXERS = 3
POOL_WINDOWS = (2, 4, 8, 16)
POOL_GROUPS = len(POOL_WINDOWS)
POOL_GROUP = D_MODEL // POOL_GROUPS
N_HEADS = 16
HEAD_DIM = D_MODEL // N_HEADS
Q_BLOCK = 128
D_RNN = D_MODEL
RNN_HEADS = 16
RNN_BLOCK = D_RNN // RNN_HEADS
CONV_WIDTH = 4
LRU_C = 8.0
D_FF = 5632
FFN_CONV_WIDTH = 3
LN_EPS = 1e-5
ALPHA = (2.0 * DEPTH) ** 0.25
BETA = (8.0 * DEPTH) ** -0.25
N_POOL_LAYERS = (DEPTH + 2) // 3
N_ATTN_LAYERS = (DEPTH + 1) // 3
N_REC_LAYERS = DEPTH // 3

kernel_name = "hybrid_pool_fox_rglru_convffn_deepnorm"


def layer_norm(x, g, b):
    xf = x.astype(jnp.float32)
    mu = jnp.mean(xf, axis=-1, keepdims=True)
    var = jnp.mean(jnp.square(xf - mu), axis=-1, keepdims=True)
    y = (xf - mu) * lax.rsqrt(var + LN_EPS)
    return (y * g.astype(jnp.float32) + b.astype(jnp.float32)).astype(x.dtype)


def causal_depthwise_conv(x, w, b):
    K = w.shape[0]
    S = x.shape[1]
    xp = jnp.pad(x, ((0, 0), (K - 1, 0), (0, 0)))
    out = b
    for k in range(K):
        out = out + xp[:, k:k + S] * w[k]
    return out


def pool_mixer(x, w, scale):
    B, S, _ = x.shape
    cs = jnp.cumsum(x.astype(jnp.float32), axis=1)
    pos = jnp.arange(1, S + 1, dtype=jnp.float32)[:, None]
    groups = []
    for g, win in enumerate(POOL_WINDOWS):
        sl = slice(g * POOL_GROUP, (g + 1) * POOL_GROUP)
        c = cs[..., sl]
        lag = jnp.pad(c, ((0, 0), (win, 0), (0, 0)))[:, :S]
        mean = (c - lag) / jnp.minimum(pos, float(win))
        groups.append(mean - x[..., sl].astype(jnp.float32))
    d = jnp.stack(groups, axis=2).astype(x.dtype)
    y = jnp.einsum('bsgc,gcd->bsgd', d, w).reshape(B, S, D_MODEL)
    return y * scale


def fox_attention(x, w_in, b_f, w_o):
    B, S, _ = x.shape
    proj = x @ w_in
    def heads(t):
        return t.reshape(B, S, N_HEADS, HEAD_DIM).transpose(0, 2, 1, 3)
    q = heads(proj[..., :D_MODEL])
    k = heads(proj[..., D_MODEL:2 * D_MODEL])
    v = heads(proj[..., 2 * D_MODEL:3 * D_MODEL])
    log_f = jax.nn.log_sigmoid((proj[..., 3 * D_MODEL:] + b_f).astype(jnp.float32))
    c = jnp.cumsum(log_f, axis=1).transpose(0, 2, 1)
    scale = HEAD_DIM ** -0.5
    outs = []
    for blk in range(S // Q_BLOCK):
        q0 = blk * Q_BLOCK
        q1 = q0 + Q_BLOCK
        qb = q[:, :, q0:q1]
        kb = k[:, :, :q1]
        vb = v[:, :, :q1]
        s = jnp.einsum('bhqd,bhkd->bhqk', qb, kb).astype(jnp.float32) * scale
        s = s + c[:, :, q0:q1, None] - c[:, :, None, :q1]
        mask = jnp.arange(q0, q1)[:, None] >= jnp.arange(q1)[None, :]
        s = jnp.where(mask, s, -jnp.inf)
        p = jax.nn.softmax(s, axis=-1).astype(v.dtype)
        outs.append(jnp.einsum('bhqk,bhkd->bhqd', p, vb))
    o = jnp.concatenate(outs, axis=2).transpose(0, 2, 1, 3).reshape(B, S, D_MODEL)
    return o @ w_o


def rglru_block(x, w_in, conv_w, conv_b, w_a, b_a, w_i, b_i, lam, w_o):
    B, S, _ = x.shape
    u = x @ w_in
    xb = u[..., :D_RNN]
    gate = jax.nn.gelu(u[..., D_RNN:], approximate=True)
    xb = causal_depthwise_conv(xb, conv_w, conv_b)
    xh = xb.reshape(B, S, RNN_HEADS, RNN_BLOCK)
    r_gate = jax.nn.sigmoid(jnp.einsum('bshc,hcd->bshd', xh, w_a).reshape(B, S, D_RNN) + b_a)
    i_gate = jax.nn.sigmoid(jnp.einsum('bshc,hcd->bshd', xh, w_i).reshape(B, S, D_RNN) + b_i)
    log_a = -LRU_C * r_gate.astype(jnp.float32) * jax.nn.softplus(-lam.astype(jnp.float32))
    a = jnp.exp(log_a)
    mult = jnp.sqrt(-jnp.expm1(2.0 * log_a))
    mult = mult.at[:, 0].set(1.0)
    bterm = mult * (i_gate * xb).astype(jnp.float32)

    def combine(left, right):
        a1, b1 = left
        a2, b2 = right
        return a1 * a2, a2 * b1 + b2

    _, h = lax.associative_scan(combine, (a, bterm), axis=1)
    y = h.astype(x.dtype) * gate
    return y @ w_o


def conv_ffn(x, w_up, conv_w, conv_b, w_down):
    h = causal_depthwise_conv(x @ w_up, conv_w, conv_b)
    g = h[..., :D_FF]
    v = h[..., D_FF:]
    return (jax.nn.silu(g) * v) @ w_down


def setup_inputs(seed: int = 0) -> dict:
    key = jax.random.key(seed)
    ks = jax.random.split(key, 24)

    def nrm(k, shape, s):
        return jax.random.normal(k, shape, jnp.float32) * s

    D = D_MODEL
    x = nrm(ks[0], (BATCH, SEQ, D), 1.0)
    pool_w = nrm(ks[1], (N_POOL_LAYERS, POOL_GROUPS, POOL_GROUP, POOL_GROUP), BETA * POOL_GROUP ** -0.5)
    pool_scale = 1.0 + nrm(ks[2], (N_POOL_LAYERS, D), 0.1)
    w_qk = nrm(ks[3], (N_ATTN_LAYERS, D, 2 * D), D ** -0.5)
    w_v = nrm(ks[4], (N_ATTN_LAYERS, D, D), BETA * D ** -0.5)
    w_f = nrm(ks[5], (N_ATTN_LAYERS, D, N_HEADS), D ** -0.5)
    attn_w_in = jnp.concatenate([w_qk, w_v, w_f], axis=-1)
    attn_b_f = 2.0 + nrm(ks[6], (N_ATTN_LAYERS, N_HEADS), 0.5)
    attn_w_o = nrm(ks[7], (N_ATTN_LAYERS, D, D), BETA * D ** -0.5)
    rec_w_in = nrm(ks[8], (N_REC_LAYERS, D, 2 * D_RNN), D ** -0.5)
    rec_conv_w = nrm(ks[9], (N_REC_LAYERS, CONV_WIDTH, D_RNN), CONV_WIDTH ** -0.5)
    rec_conv_b = nrm(ks[10], (N_REC_LAYERS, D_RNN), 0.01)
    rec_w_a = nrm(ks[11], (N_REC_LAYERS, RNN_HEADS, RNN_BLOCK, RNN_BLOCK), RNN_BLOCK ** -0.5)
    rec_b_a = nrm(ks[12], (N_REC_LAYERS, D_RNN), 0.01)
    rec_w_i = nrm(ks[13], (N_REC_LAYERS, RNN_HEADS, RNN_BLOCK, RNN_BLOCK), RNN_BLOCK ** -0.5)
    rec_b_i = nrm(ks[14], (N_REC_LAYERS, D_RNN), 0.01)
    a_c = jax.random.uniform(ks[15], (N_REC_LAYERS, D_RNN), jnp.float32, 0.9, 0.999)
    sig = a_c ** (1.0 / LRU_C)
    rec_lam = jnp.log(sig) - jnp.log1p(-sig)
    rec_w_o = nrm(ks[16], (N_REC_LAYERS, D_RNN, D), BETA * D_RNN ** -0.5)
    ln_g = 1.0 + nrm(ks[17], (DEPTH, 2, D), 0.05)
    ln_b = nrm(ks[18], (DEPTH, 2, D), 0.02)
    ffn_w_up = nrm(ks[19], (DEPTH, D, 2 * D_FF), D ** -0.5)
    ffn_conv_w = nrm(ks[20], (DEPTH, FFN_CONV_WIDTH, 2 * D_FF), FFN_CONV_WIDTH ** -0.5)
    ffn_conv_b = nrm(ks[21], (DEPTH, 2 * D_FF), 0.01)
    ffn_w_down = nrm(ks[22], (DEPTH, D_FF, D), BETA * D_FF ** -0.5)
    return {"x": x, "pool_w": pool_w, "pool_scale": pool_scale,
            "attn_w_in": attn_w_in, "attn_b_f": attn_b_f, "attn_w_o": attn_w_o,
            "rec_w_in": rec_w_in, "rec_conv_w": rec_conv_w, "rec_conv_b": rec_conv_b,
            "rec_w_a": rec_w_a, "rec_b_a": rec_b_a, "rec_w_i": rec_w_i, "rec_b_i": rec_b_i,
            "rec_lam": rec_lam, "rec_w_o": rec_w_o, "ln_g": ln_g, "ln_b": ln_b,
            "ffn_w_up": ffn_w_up, "ffn_conv_w": ffn_conv_w, "ffn_conv_b": ffn_conv_b,
            "ffn_w_down": ffn_w_down}


def reference(x, pool_w, pool_scale, attn_w_in, attn_b_f, attn_w_o, rec_w_in, rec_conv_w, rec_conv_b,
              rec_w_a, rec_b_a, rec_w_i, rec_b_i, rec_lam, rec_w_o, ln_g, ln_b,
              ffn_w_up, ffn_conv_w, ffn_conv_b, ffn_w_down):
    for layer in range(DEPTH):
        kind = layer % N_MIXERS
        j = layer // N_MIXERS
        if kind == 0:
            m = pool_mixer(x, pool_w[j], pool_scale[j])
        elif kind == 1:
            m = fox_attention(x, attn_w_in[j], attn_b_f[j], attn_w_o[j])
        else:
            m = rglru_block(x, rec_w_in[j], rec_conv_w[j], rec_conv_b[j], rec_w_a[j], rec_b_a[j],
                            rec_w_i[j], rec_b_i[j], rec_lam[j], rec_w_o[j])
        x = layer_norm(ALPHA * x + m, ln_g[layer, 0], ln_b[layer, 0])
        f = conv_ffn(x, ffn_w_up[layer], ffn_conv_w[layer], ffn_conv_b[layer], ffn_w_down[layer])
        x = layer_norm(ALPHA * x + f, ln_g[layer, 1], ln_b[layer, 1])
    return x
```

```python
import functools

import jax
import jax.numpy as jnp
from jax import lax
from jax.experimental import pallas as pl
from jax.experimental.pallas import tpu as pltpu

LN_EPS = 1e-5
POOL_WINDOWS = (2, 4, 8, 16)
N_HEADS = 16
RNN_HEADS = 16
LRU_C = 8.0
N_MIXERS = 3

LANES = 128
SUBLANES = 8
POOL_HALO = 16
VMEM_LIMIT = 56 * 1024 * 1024

F32 = jnp.float32
BF16 = jnp.bfloat16


def _params(*semantics):
    return pltpu.CompilerParams(dimension_semantics=semantics, vmem_limit_bytes=VMEM_LIMIT)


def _dot(a, b):
    return jnp.dot(a, b, preferred_element_type=F32)


def _layer_norm(y, g, b):
    mu = jnp.mean(y, axis=-1, keepdims=True)
    yc = y - mu
    var = jnp.mean(yc * yc, axis=-1, keepdims=True)
    return yc * lax.rsqrt(var + LN_EPS) * g + b


def _store_residual_ln(y, g_ref, b_ref, o_ref, ob_ref):
    out = _layer_norm(y, g_ref[...], b_ref[...])
    o_ref[...] = out.reshape(o_ref.shape)
    ob_ref[...] = out.astype(BF16).reshape(ob_ref.shape)


def _pool_kernel(x_ref, halo_ref, w_ref, scale_ref, g_ref, b_ref, o_ref, ob_ref, xe_ref, m_ref,
                 *, tm, alpha):
    i = pl.program_id(1)
    d_model = x_ref.shape[-1]
    cg = d_model // len(POOL_WINDOWS)
    xe_ref[POOL_HALO:, :] = x_ref[0]

    @pl.when(i == 0)
    def _():
        xe_ref[:POOL_HALO, :] = jnp.zeros((POOL_HALO, d_model), F32)

    @pl.when(i > 0)
    def _():
        xe_ref[:POOL_HALO, :] = halo_ref[0]

    pos = i * tm + lax.broadcasted_iota(jnp.int32, (tm, 1), 0) + 1
    for g, win in enumerate(POOL_WINDOWS):
        cols = slice(g * cg, (g + 1) * cg)
        xg = x_ref[0, :, cols]
        s = xg
        for k in range(1, win):
            s = s + xe_ref[POOL_HALO - k:POOL_HALO - k + tm, cols]
        denom = jnp.minimum(pos, win).astype(F32)
        d = s / denom - xg
        m_ref[:, cols] = _dot(d.astype(BF16), w_ref[g])
    y = alpha * x_ref[0] + m_ref[...] * scale_ref[...]
    _store_residual_ln(y, g_ref, b_ref, o_ref, ob_ref)


def _pool_layer(x, w, scale, ln_g, ln_b, alpha, tm=512):
    bsz, seq, d = x.shape
    tm = min(tm, seq)
    n_groups = len(POOL_WINDOWS)
    cg = d // n_groups
    hblk = tm // POOL_HALO
    row = lambda b, i: (b, i, 0)
    const2 = lambda b, i: (0, 0)
    return pl.pallas_call(
        functools.partial(_pool_kernel, tm=tm, alpha=alpha),
        grid=(bsz, seq // tm),
        in_specs=[
            pl.BlockSpec((1, tm, d), row),
            pl.BlockSpec((1, POOL_HALO, d), lambda b, i: (b, jnp.maximum(i * hblk - 1, 0), 0)),
            pl.BlockSpec((n_groups, cg, cg), lambda b, i: (0, 0, 0)),
            pl.BlockSpec((1, d), const2),
            pl.BlockSpec((1, d), const2),
            pl.BlockSpec((1, d), const2),
        ],
        out_specs=[pl.BlockSpec((1, tm, d), row), pl.BlockSpec((1, tm, d), row)],
        out_shape=[jax.ShapeDtypeStruct(x.shape, F32), jax.ShapeDtypeStruct(x.shape, BF16)],
        scratch_shapes=[pltpu.VMEM((tm + POOL_HALO, d), F32), pltpu.VMEM((tm, d), F32)],
        compiler_params=_params("parallel", "arbitrary"),
        name="pool_mixer_ln",
    )(x, x, w.astype(BF16), scale.reshape(1, d), ln_g.reshape(1, d), ln_b.reshape(1, d))


def _ffn_kernel(xb_ref, x_ref, wg_ref, wv_ref, cwg_ref, cwv_ref, cbg_ref, cbv_ref, wd_ref,
                g_ref, b_ref, o_ref, ob_ref, acc_ref, hg_ref, hv_ref, carry_ref,
                *, tm, fc, tiles_per_seq, alpha):
    i = pl.program_id(0)
    j = pl.program_id(1)
    seq_start = (i % tiles_per_seq) == 0

    @pl.when(seq_start)
    def _():
        hg_ref[:SUBLANES, :] = jnp.zeros((SUBLANES, fc), F32)
        hv_ref[:SUBLANES, :] = jnp.zeros((SUBLANES, fc), F32)

    @pl.when(jnp.logical_not(seq_start))
    def _():
        hg_ref[:SUBLANES, :] = carry_ref[j, :, :fc]
        hv_ref[:SUBLANES, :] = carry_ref[j, :, fc:]

    xb = xb_ref[...]
    hg_ref[SUBLANES:, :] = _dot(xb, wg_ref[...])
    hv_ref[SUBLANES:, :] = _dot(xb, wv_ref[...])
    carry_ref[j, :, :fc] = hg_ref[tm:, :]
    carry_ref[j, :, fc:] = hv_ref[tm:, :]

    def conv(h_ref, cw_ref, cb_ref):
        out = cb_ref[...] + cw_ref[0:1, :] * h_ref[SUBLANES - 2:SUBLANES - 2 + tm, :]
        out = out + cw_ref[1:2, :] * h_ref[SUBLANES - 1:SUBLANES - 1 + tm, :]
        return out + cw_ref[2:3, :] * h_ref[SUBLANES:, :]

    gate = conv(hg_ref, cwg_ref, cbg_ref)
    val = conv(hv_ref, cwv_ref, cbv_ref)
    act = (gate * jax.nn.sigmoid(gate) * val).astype(BF16)

    @pl.when(j == 0)
    def _():
        acc_ref[...] = jnp.zeros_like(acc_ref)

    acc_ref[...] += _dot(act, wd_ref[...])

    @pl.when(j == pl.num_programs(1) - 1)
    def _():
        _store_residual_ln(alpha * x_ref[...] + acc_ref[...], g_ref, b_ref, o_ref, ob_ref)


def _ffn_layer(x, xb, w_up, conv_w, conv_b, w_down, ln_g, ln_b, alpha, seq, tm=512, fc=512):
    n, d = x.shape
    f = w_down.shape[0]
    tm = min(tm, seq)
    fc = min(fc, f)
    nj = f // fc
    row = lambda i, j: (i, 0)
    const2 = lambda i, j: (0, 0)
    gcol = lambda i, j: (0, j)
    vcol = lambda i, j: (0, nj + j)
    w_up = w_up.astype(BF16)
    conv_b = conv_b.reshape(1, 2 * f)
    return pl.pallas_call(
        functools.partial(_ffn_kernel, tm=tm, fc=fc, tiles_per_seq=seq // tm, alpha=alpha),
        grid=(n // tm, nj),
        in_specs=[
            pl.BlockSpec((tm, d), row),
            pl.BlockSpec((tm, d), row),
            pl.BlockSpec((d, fc), gcol),
            pl.BlockSpec((d, fc), vcol),
            pl.BlockSpec((conv_w.shape[0], fc), gcol),
            pl.BlockSpec((conv_w.shape[0], fc), vcol),
            pl.BlockSpec((1, fc), gcol),
            pl.BlockSpec((1, fc), vcol),
            pl.BlockSpec((fc, d), lambda i, j: (j, 0)),
            pl.BlockSpec((1, d), const2),
            pl.BlockSpec((1, d), const2),
        ],
        out_specs=[pl.BlockSpec((tm, d), row), pl.BlockSpec((tm, d), row)],
        out_shape=[jax.ShapeDtypeStruct((n, d), F32), jax.ShapeDtypeStruct((n, d), BF16)],
        scratch_shapes=[
            pltpu.VMEM((tm, d), F32),
            pltpu.VMEM((tm + SUBLANES, fc), F32),
            pltpu.VMEM((tm + SUBLANES, fc), F32),
            pltpu.VMEM((nj, SUBLANES, 2 * fc), F32),
        ],
        compiler_params=_params("arbitrary", "arbitrary"),
        name="conv_ffn_ln",
    )(xb, x, w_up, w_up, conv_w, conv_w, conv_b, conv_b, w_down.astype(BF16),
      ln_g.reshape(1, d), ln_b.reshape(1, d))


def _proj_ln_kernel(y_ref, x_ref, w_ref, g_ref, b_ref, o_ref, ob_ref, *, alpha):
    m = _dot(y_ref[...], w_ref[...])
    _store_residual_ln(alpha * x_ref[...] + m, g_ref, b_ref, o_ref, ob_ref)


def _proj_ln_layer(y, x, w, ln_g, ln_b, alpha, tm=512):
    n, d = x.shape
    k = y.shape[1]
    tm = min(tm, n)
    row = lambda i: (i, 0)
    const = lambda i: (0, 0)
    return pl.pallas_call(
        functools.partial(_proj_ln_kernel, alpha=alpha),
        grid=(n // tm,),
        in_specs=[
            pl.BlockSpec((tm, k), row),
            pl.BlockSpec((tm, d), row),
            pl.BlockSpec((k, d), const),
            pl.BlockSpec((1, d), const),
            pl.BlockSpec((1, d), const),
        ],
        out_specs=[pl.BlockSpec((tm, d), row), pl.BlockSpec((tm, d), row)],
        out_shape=[jax.ShapeDtypeStruct((n, d), F32), jax.ShapeDtypeStruct((n, d), BF16)],
        compiler_params=_params("parallel"),
        name="out_proj_ln",
    )(y, x, w.astype(BF16), ln_g.reshape(1, d), ln_b.reshape(1, d))


def _qkv_kernel(x_ref, w_ref, o_ref, *, n_q_blocks, scale):
    j = pl.program_id(1)
    s = jnp.where(j < n_q_blocks, scale, 1.0).astype(F32)
    o_ref[...] = (_dot(x_ref[...], w_ref[...]) * s).astype(BF16)


def _qkv_proj(xb, w_qkv, d_model, scale, tm=1024, tn=1024):
    n, d = xb.shape
    nout = w_qkv.shape[1]
    tm = min(tm, n)
    return pl.pallas_call(
        functools.partial(_qkv_kernel, n_q_blocks=d_model // tn, scale=scale),
        grid=(n // tm, nout // tn),
        in_specs=[pl.BlockSpec((tm, d), lambda i, j: (i, 0)), pl.BlockSpec((d, tn), lambda i, j: (0, j))],
        out_specs=pl.BlockSpec((tm, tn), lambda i, j: (i, j)),
        out_shape=jax.ShapeDtypeStruct((n, nout), BF16),
        compiler_params=_params("parallel", "arbitrary"),
        name="qkv_proj",
    )(xb, w_qkv)


def _fgate_kernel(x_ref, wf_ref, bf_ref, c_ref, ct_ref):
    z = _dot(x_ref[0], wf_ref[...]) + bf_ref[...]
    c = jnp.minimum(z, 0.0) - jnp.log1p(jnp.exp(-jnp.abs(z)))
    seq = c.shape[0]
    row = lax.broadcasted_iota(jnp.int32, c.shape, 0)
    k = 1
    while k < seq:
        c = c + jnp.where(row >= k, pltpu.roll(c, k, 0), 0.0)
        k *= 2
    c_ref[0] = c
    ct_ref[0] = c.T


def _fgate_cumsum(xb3, w_f, b_f):
    bsz, seq, d = xb3.shape
    wf = jnp.zeros((d, LANES), F32).at[:, :N_HEADS].set(w_f).astype(BF16)
    bf = jnp.zeros((1, LANES), F32).at[0, :N_HEADS].set(b_f)
    return pl.pallas_call(
        _fgate_kernel,
        grid=(bsz,),
        in_specs=[
            pl.BlockSpec((1, seq, d), lambda b: (b, 0, 0)),
            pl.BlockSpec((d, LANES), lambda b: (0, 0)),
            pl.BlockSpec((1, LANES), lambda b: (0, 0)),
        ],
        out_specs=[pl.BlockSpec((1, seq, LANES), lambda b: (b, 0, 0)),
                   pl.BlockSpec((1, LANES, seq), lambda b: (b, 0, 0))],
        out_shape=[jax.ShapeDtypeStruct((bsz, seq, LANES), F32),
                   jax.ShapeDtypeStruct((bsz, LANES, seq), F32)],
        compiler_params=_params("parallel"),
        name="forget_gate_cumsum",
    )(xb3, wf, bf)


def _flash_kernel(q_ref, k_ref, v_ref, c_ref, ct_ref, o_ref, *, tq):
    h = pl.program_id(1)
    seq = q_ref.shape[1]
    lane = lax.broadcasted_iota(jnp.int32, (seq, LANES), 1)
    cq_all = jnp.sum(jnp.where(lane == h, c_ref[0], 0.0), axis=-1, keepdims=True)
    rows = lax.broadcasted_iota(jnp.int32, (tq, tq), 0)
    cols = lax.broadcasted_iota(jnp.int32, (tq, tq), 1)
    causal = cols <= rows

    for qi in range(seq // tq):
        q = q_ref[0, qi * tq:(qi + 1) * tq, :]
        cq = cq_all[qi * tq:(qi + 1) * tq]

        def block(start, carry, masked, q=q, cq=cq):
            m, l, acc = carry
            k = k_ref[0, pl.ds(start, tq), :]
            v = v_ref[0, pl.ds(start, tq), :]
            s = lax.dot_general(q, k, (((1,), (1,)), ((), ())), preferred_element_type=F32)
            s = s + (cq - ct_ref[0, pl.ds(h, 1), pl.ds(start, tq)])
            if masked:
                s = jnp.where(causal, s, -jnp.inf)
            m_new = jnp.maximum(m, jnp.max(s, axis=-1, keepdims=True))
            p = jnp.exp(s - m_new)
            a = jnp.exp(m - m_new)
            l = a * l + jnp.sum(p, axis=-1, keepdims=True)
            acc = a * acc + _dot(p.astype(BF16), v)
            return m_new, l, acc

        init = (jnp.full((tq, 1), -jnp.inf, F32), jnp.zeros((tq, 1), F32),
                jnp.zeros((tq, q_ref.shape[2]), F32))
        carry = lax.fori_loop(
            0, qi, lambda kb, c: block(pl.multiple_of(kb * tq, tq), c, False), init)
        _, l, acc = block(qi * tq, carry, True)
        o_ref[0, qi * tq:(qi + 1) * tq, :] = (acc / l).astype(BF16)


def _flash_attention(qkv3, c, ct, d_model, tq=512):
    bsz, seq, _ = qkv3.shape
    dh = d_model // N_HEADS
    tq = min(tq, seq)
    return pl.pallas_call(
        functools.partial(_flash_kernel, tq=tq),
        grid=(bsz, N_HEADS),
        in_specs=[
            pl.BlockSpec((1, seq, dh), lambda b, h: (b, 0, h)),
            pl.BlockSpec((1, seq, dh), lambda b, h: (b, 0, N_HEADS + h)),
            pl.BlockSpec((1, seq, dh), lambda b, h: (b, 0, 2 * N_HEADS + h)),
            pl.BlockSpec((1, seq, LANES), lambda b, h: (b, 0, 0)),
            pl.BlockSpec((1, N_HEADS, seq), lambda b, h: (b, 0, 0)),
        ],
        out_specs=pl.BlockSpec((1, seq, dh), lambda b, h: (b, 0, h)),
        out_shape=jax.ShapeDtypeStruct((bsz, seq, d_model), BF16),
        compiler_params=_params("parallel", "arbitrary"),
        name="fox_flash_attention",
    )(qkv3, qkv3, qkv3, c, ct)


def _attention_layer(x, xb, w_in, b_f, w_o, ln_g, ln_b, alpha, bsz, seq):
    n, d = x.shape
    dh = d // N_HEADS
    qkv = _qkv_proj(xb, w_in[:, :3 * d].astype(BF16), d, dh ** -0.5)
    c, ct = _fgate_cumsum(xb.reshape(bsz, seq, d), w_in[:, 3 * d:], b_f)
    o = _flash_attention(qkv.reshape(bsz, seq, 3 * d), c, ct, d)
    return _proj_ln_layer(o.reshape(n, d), x, w_o, ln_g, ln_b, alpha)


def _rec_in_kernel(x_ref, w1_ref, w2_ref, o1_ref, o2_ref):
    x = x_ref[...]
    o1_ref[...] = _dot(x, w1_ref[...])
    o2_ref[...] = jax.nn.gelu(_dot(x, w2_ref[...]), approximate=True)


def _rec_in_proj(xb, w_in, tm=1024, tn=1024):
    n, d = xb.shape
    d_rnn = w_in.shape[1] // 2
    tm = min(tm, n)
    nj = d_rnn // tn
    w_in = w_in.astype(BF16)
    return pl.pallas_call(
        _rec_in_kernel,
        grid=(n // tm, nj),
        in_specs=[pl.BlockSpec((tm, d), lambda i, j: (i, 0)),
                  pl.BlockSpec((d, tn), lambda i, j: (0, j)),
                  pl.BlockSpec((d, tn), lambda i, j: (0, nj + j))],
        out_specs=[pl.BlockSpec((tm, tn), lambda i, j: (i, j)), pl.BlockSpec((tm, tn), lambda i, j: (i, j))],
        out_shape=[jax.ShapeDtypeStruct((n, d_rnn), F32), jax.ShapeDtypeStruct((n, d_rnn), F32)],
        compiler_params=_params("parallel", "arbitrary"),
        name="rglru_in_proj",
    )(xb, w_in, w_in)


def _softplus(z):
    return jnp.maximum(z, 0.0) + jnp.log1p(jnp.exp(-jnp.abs(z)))


def _rglru_kernel(xr_ref, halo_ref, gate_ref, cw_ref, cb_ref, wa_ref, ba_ref, wi_ref, bi_ref, lam_ref,
                  y_ref, xe_ref, a_ref, b_ref, hc_ref, *, tm, conv_width):
    i = pl.program_id(1)
    d_rnn = xr_ref.shape[-1]
    blk = d_rnn // RNN_HEADS
    xe_ref[SUBLANES:, :] = xr_ref[0]

    @pl.when(i == 0)
    def _():
        xe_ref[:SUBLANES, :] = jnp.zeros((SUBLANES, d_rnn), F32)
        hc_ref[...] = jnp.zeros_like(hc_ref)

    @pl.when(i > 0)
    def _():
        xe_ref[:SUBLANES, :] = halo_ref[0]

    first_row = jnp.logical_and(i == 0, lax.broadcasted_iota(jnp.int32, (tm, blk), 0) == 0)
    for h in range(RNN_HEADS):
        cols = slice(h * blk, (h + 1) * blk)
        xb = cb_ref[:, cols]
        for k in range(conv_width):
            off = SUBLANES - (conv_width - 1) + k
            xb = xb + cw_ref[k:k + 1, cols] * xe_ref[off:off + tm, cols]
        xb16 = xb.astype(BF16)
        r = jax.nn.sigmoid(_dot(xb16, wa_ref[h]) + ba_ref[:, cols])
        ig = jax.nn.sigmoid(_dot(xb16, wi_ref[h]) + bi_ref[:, cols])
        log_a = (-LRU_C) * r * _softplus(-lam_ref[:, cols])
        a = jnp.exp(log_a)
        mult = jnp.sqrt(-jnp.tanh(log_a) * (a * a + 1.0))
        mult = jnp.where(first_row, 1.0, mult)
        a_ref[:, cols] = a
        b_ref[:, cols] = mult * (ig * xb)

    row = lax.broadcasted_iota(jnp.int32, (SUBLANES, d_rnn), 0)

    def slab(s, hprev):
        start = pl.multiple_of(s * SUBLANES, SUBLANES)
        a = a_ref[pl.ds(start, SUBLANES), :]
        b = b_ref[pl.ds(start, SUBLANES), :]
        for dist in (1, 2, 4):
            keep = row >= dist
            b = jnp.where(keep, a * pltpu.roll(b, dist, 0) + b, b)
            a = jnp.where(keep, a * pltpu.roll(a, dist, 0), a)
        hcur = a * hprev + b
        b_ref[pl.ds(start, SUBLANES), :] = hcur
        return jnp.broadcast_to(hcur[SUBLANES - 1:SUBLANES, :], (SUBLANES, d_rnn))

    hc_ref[...] = lax.fori_loop(0, tm // SUBLANES, slab, hc_ref[...])
    y_ref[0] = (b_ref[...] * gate_ref[0]).astype(BF16)


def _rglru_core(xr3, gate3, conv_w, conv_b, w_a, b_a, w_i, b_i, lam, tm=256):
    bsz, seq, d = xr3.shape
    tm = min(tm, seq)
    hblk = tm // SUBLANES
    blk = d // RNN_HEADS
    row = lambda b, i: (b, i, 0)
    const2 = lambda b, i: (0, 0)
    const3 = lambda b, i: (0, 0, 0)
    vec = lambda v: v.reshape(1, d)
    return pl.pallas_call(
        functools.partial(_rglru_kernel, tm=tm, conv_width=conv_w.shape[0]),
        grid=(bsz, seq // tm),
        in_specs=[
            pl.BlockSpec((1, tm, d), row),
            pl.BlockSpec((1, SUBLANES, d), lambda b, i: (b, jnp.maximum(i * hblk - 1, 0), 0)),
            pl.BlockSpec((1, tm, d), row),
            pl.BlockSpec((conv_w.shape[0], d), const2),
            pl.BlockSpec((1, d), const2),
            pl.BlockSpec((RNN_HEADS, blk, blk), const3),
            pl.BlockSpec((1, d), const2),
            pl.BlockSpec((RNN_HEADS, blk, blk), const3),
            pl.BlockSpec((1, d), const2),
            pl.BlockSpec((1, d), const2),
        ],
        out_specs=pl.BlockSpec((1, tm, d), row),
        out_shape=jax.ShapeDtypeStruct((bsz, seq, d), BF16),
        scratch_shapes=[
            pltpu.VMEM((tm + SUBLANES, d), F32),
            pltpu.VMEM((tm, d), F32),
            pltpu.VMEM((tm, d), F32),
            pltpu.VMEM((SUBLANES, d), F32),
        ],
        compiler_params=_params("parallel", "arbitrary"),
        name="rglru_scan",
    )(xr3, xr3, gate3, conv_w, vec(conv_b), w_a.astype(BF16), vec(b_a), w_i.astype(BF16), vec(b_i), vec(lam))


def _rglru_layer(x, xb, w_in, conv_w, conv_b, w_a, b_a, w_i, b_i, lam, w_o, ln_g, ln_b, alpha, bsz, seq):
    n, d = x.shape
    xr, gate = _rec_in_proj(xb, w_in)
    d_rnn = xr.shape[1]
    y = _rglru_core(xr.reshape(bsz, seq, d_rnn), gate.reshape(bsz, seq, d_rnn),
                    conv_w, conv_b, w_a, b_a, w_i, b_i, lam)
    return _proj_ln_layer(y.reshape(n, d_rnn), x, w_o, ln_g, ln_b, alpha)


def kernel(x, pool_w, pool_scale, attn_w_in, attn_b_f, attn_w_o, rec_w_in, rec_conv_w, rec_conv_b, rec_w_a, rec_b_a, rec_w_i, rec_b_i, rec_lam, rec_w_o, ln_g, ln_b, ffn_w_up, ffn_conv_w, ffn_conv_b, ffn_w_down):
    bsz, seq, d = x.shape
    depth = ln_g.shape[0]
    alpha = (2.0 * depth) ** 0.25
    n = bsz * seq
    xf = x
    xb = None
    for layer in range(depth):
        kind = layer % N_MIXERS
        j = layer // N_MIXERS
        if kind == 0:
            xf3, xb3 = _pool_layer(xf.reshape(bsz, seq, d), pool_w[j], pool_scale[j],
                                   ln_g[layer, 0], ln_b[layer, 0], alpha)
            xf, xb = xf3.reshape(n, d), xb3.reshape(n, d)
        elif kind == 1:
            xf, xb = _attention_layer(xf, xb, attn_w_in[j], attn_b_f[j], attn_w_o[j],
                                      ln_g[layer, 0], ln_b[layer, 0], alpha, bsz, seq)
        else:
            xf, xb = _rglru_layer(xf, xb, rec_w_in[j], rec_conv_w[j], rec_conv_b[j], rec_w_a[j], rec_b_a[j],
                                  rec_w_i[j], rec_b_i[j], rec_lam[j], rec_w_o[j],
                                  ln_g[layer, 0], ln_b[layer, 0], alpha, bsz, seq)
        xf, xb = _ffn_layer(xf, xb, ffn_w_up[layer], ffn_conv_w[layer], ffn_conv_b[layer], ffn_w_down[layer],
                            ln_g[layer, 1], ln_b[layer, 1], alpha, seq)
    return xf.reshape(bsz, seq, d)
```

```python
import functools

import jax
import jax.numpy as jnp
from jax import lax
from jax.experimental import pallas as pl
from jax.experimental.pallas import tpu as pltpu

LN_EPS = 1e-5
POOL_WINDOWS = (2, 4, 8, 16)
N_HEADS = 16
RNN_HEADS = 16
LRU_C = 8.0
N_MIXERS = 3

LANES = 128
SUBLANES = 8
POOL_HALO = 16
VMEM_LIMIT = 56 * 1024 * 1024

F32 = jnp.float32
BF16 = jnp.bfloat16


def _params(*semantics):
    return pltpu.CompilerParams(dimension_semantics=semantics, vmem_limit_bytes=VMEM_LIMIT)


def _dot(a, b):
    return jnp.dot(a, b, preferred_element_type=F32)


def _layer_norm(y, g, b):
    mu = jnp.mean(y, axis=-1, keepdims=True)
    yc = y - mu
    var = jnp.mean(yc * yc, axis=-1, keepdims=True)
    return yc * lax.rsqrt(var + LN_EPS) * g + b


def _store_residual_ln(y, g_ref, b_ref, o_ref, ob_ref):
    out = _layer_norm(y, g_ref[...], b_ref[...])
    o_ref[...] = out.reshape(o_ref.shape)
    ob_ref[...] = out.astype(BF16).reshape(ob_ref.shape)


def _pool_kernel(x_ref, halo_ref, w_ref, scale_ref, g_ref, b_ref, o_ref, ob_ref, xe_ref, m_ref,
                 *, tm, alpha):
    i = pl.program_id(1)
    d_model = x_ref.shape[-1]
    n_slabs = d_model // LANES
    slabs_per_group = n_slabs // len(POOL_WINDOWS)
    cg = d_model // len(POOL_WINDOWS)
    for c in range(n_slabs):
        xe_ref[c, POOL_HALO:, :] = x_ref[0, :, c * LANES:(c + 1) * LANES]

    @pl.when(i == 0)
    def _():
        xe_ref[:, :POOL_HALO, :] = jnp.zeros((n_slabs, POOL_HALO, LANES), F32)

    @pl.when(i > 0)
    def _():
        for c in range(n_slabs):
            xe_ref[c, :POOL_HALO, :] = halo_ref[0, :, c * LANES:(c + 1) * LANES]

    pos = i * tm + lax.broadcasted_iota(jnp.int32, (tm, 1), 0) + 1
    for g, win in enumerate(POOL_WINDOWS):
        cols = slice(g * cg, (g + 1) * cg)
        inv = 1.0 / jnp.minimum(pos, win).astype(F32)
        ds = []
        for c in range(g * slabs_per_group, (g + 1) * slabs_per_group):
            xc = xe_ref[c, POOL_HALO:, :]
            s = xc
            for k in range(1, win):
                s = s + xe_ref[c, pl.ds(POOL_HALO - k, tm, stride=1), :]
            ds.append((s * inv - xc).astype(BF16))
        m_ref[:, cols] = _dot(jnp.concatenate(ds, axis=1), w_ref[g])
    y = alpha * x_ref[0] + m_ref[...] * scale_ref[...]
    _store_residual_ln(y, g_ref, b_ref, o_ref, ob_ref)


def _pool_layer(x, w, scale, ln_g, ln_b, alpha, tm=512):
    bsz, seq, d = x.shape
    tm = min(tm, seq)
    n_groups = len(POOL_WINDOWS)
    cg = d // n_groups
    hblk = tm // POOL_HALO
    row = lambda b, i: (b, i, 0)
    const2 = lambda b, i: (0, 0)
    return pl.pallas_call(
        functools.partial(_pool_kernel, tm=tm, alpha=alpha),
        grid=(bsz, seq // tm),
        in_specs=[
            pl.BlockSpec((1, tm, d), row),
            pl.BlockSpec((1, POOL_HALO, d), lambda b, i: (b, jnp.maximum(i * hblk - 1, 0), 0)),
            pl.BlockSpec((n_groups, cg, cg), lambda b, i: (0, 0, 0)),
            pl.BlockSpec((1, d), const2),
            pl.BlockSpec((1, d), const2),
            pl.BlockSpec((1, d), const2),
        ],
        out_specs=[pl.BlockSpec((1, tm, d), row), pl.BlockSpec((1, tm, d), row)],
        out_shape=[jax.ShapeDtypeStruct(x.shape, F32), jax.ShapeDtypeStruct(x.shape, BF16)],
        scratch_shapes=[pltpu.VMEM((d // LANES, tm + POOL_HALO, LANES), F32), pltpu.VMEM((tm, d), F32)],
        compiler_params=_params("parallel", "arbitrary"),
        name="pool_mixer_ln",
    )(x, x, w.astype(BF16), scale.reshape(1, d), ln_g.reshape(1, d), ln_b.reshape(1, d))


def _ffn_kernel(xb_ref, x_ref, wg_ref, wv_ref, cwg_ref, cwv_ref, cbg_ref, cbv_ref, wd_ref,
                g_ref, b_ref, o_ref, ob_ref, acc_ref, hg_ref, hv_ref, carry_ref,
                *, tm, fc, n_sub, tiles_per_seq, alpha):
    i = pl.program_id(0)
    j = pl.program_id(1)
    seq_start = (i % tiles_per_seq) == 0
    taps = cwg_ref.shape[0]

    @pl.when(seq_start)
    def _():
        hg_ref[:SUBLANES, :] = jnp.zeros((SUBLANES, fc), F32)
        hv_ref[:SUBLANES, :] = jnp.zeros((SUBLANES, fc), F32)

    @pl.when(jnp.logical_not(seq_start))
    def _():
        hg_ref[:SUBLANES, :] = carry_ref[j, :, :fc]
        hv_ref[:SUBLANES, :] = carry_ref[j, :, fc:]

    @pl.when(j == 0)
    def _():
        acc_ref[...] = jnp.zeros_like(acc_ref)

    tr = tm // n_sub
    for r in range(n_sub):
        xb = xb_ref[r * tr:(r + 1) * tr, :]
        hg_ref[SUBLANES + r * tr:SUBLANES + (r + 1) * tr, :] = _dot(xb, wg_ref[...])
        hv_ref[SUBLANES + r * tr:SUBLANES + (r + 1) * tr, :] = _dot(xb, wv_ref[...])
    carry_ref[j, :, :fc] = hg_ref[tm:, :]
    carry_ref[j, :, fc:] = hv_ref[tm:, :]

    def conv(h_ref, cw_ref, cb_ref, r0):
        out = cb_ref[...]
        for k in range(taps):
            off = r0 + SUBLANES - (taps - 1) + k
            out = out + cw_ref[k:k + 1, :] * h_ref[off:off + tr, :]
        return out

    for r in range(n_sub):
        gate = conv(hg_ref, cwg_ref, cbg_ref, r * tr)
        val = conv(hv_ref, cwv_ref, cbv_ref, r * tr)
        act = (gate * jax.nn.sigmoid(gate) * val).astype(BF16)
        acc_ref[r * tr:(r + 1) * tr, :] += _dot(act, wd_ref[...])

    @pl.when(j == pl.num_programs(1) - 1)
    def _():
        _store_residual_ln(alpha * x_ref[...] + acc_ref[...], g_ref, b_ref, o_ref, ob_ref)


def _ffn_layer(x, xb, w_up, conv_w, conv_b, w_down, ln_g, ln_b, layer, alpha, seq, tm=512, fc=512, n_sub=2):
    n, d = x.shape
    f = w_down.shape[1]
    tm = min(tm, seq)
    fc = min(fc, f)
    nj = f // fc
    row = lambda i, j: (i, 0)
    gcol = lambda i, j: (layer, 0, j)
    vcol = lambda i, j: (layer, 0, nj + j)
    ln_row = lambda i, j: (2 * layer + 1, 0, 0)
    return pl.pallas_call(
        functools.partial(_ffn_kernel, tm=tm, fc=fc, n_sub=n_sub, tiles_per_seq=seq // tm, alpha=alpha),
        grid=(n // tm, nj),
        in_specs=[
            pl.BlockSpec((tm, d), row),
            pl.BlockSpec((tm, d), row),
            pl.BlockSpec((None, d, fc), gcol),
            pl.BlockSpec((None, d, fc), vcol),
            pl.BlockSpec((None, conv_w.shape[1], fc), gcol),
            pl.BlockSpec((None, conv_w.shape[1], fc), vcol),
            pl.BlockSpec((None, 1, fc), gcol),
            pl.BlockSpec((None, 1, fc), vcol),
            pl.BlockSpec((None, fc, d), lambda i, j: (layer, j, 0)),
            pl.BlockSpec((None, 1, d), ln_row),
            pl.BlockSpec((None, 1, d), ln_row),
        ],
        out_specs=[pl.BlockSpec((tm, d), row), pl.BlockSpec((tm, d), row)],
        out_shape=[jax.ShapeDtypeStruct((n, d), F32), jax.ShapeDtypeStruct((n, d), BF16)],
        scratch_shapes=[
            pltpu.VMEM((tm, d), F32),
            pltpu.VMEM((tm + SUBLANES, fc), F32),
            pltpu.VMEM((tm + SUBLANES, fc), F32),
            pltpu.VMEM((nj, SUBLANES, 2 * fc), F32),
        ],
        compiler_params=_params("arbitrary", "arbitrary"),
        name="conv_ffn_ln",
    )(xb, x, w_up, w_up, conv_w, conv_w, conv_b, conv_b, w_down, ln_g, ln_b)


def _proj_ln_kernel(y_ref, x_ref, w_ref, g_ref, b_ref, o_ref, ob_ref, *, alpha):
    m = _dot(y_ref[...], w_ref[...])
    _store_residual_ln(alpha * x_ref[...] + m, g_ref, b_ref, o_ref, ob_ref)


def _proj_ln_layer(y, x, w, ln_g, ln_b, alpha, tm=512):
    n, d = x.shape
    k = y.shape[1]
    tm = min(tm, n)
    row = lambda i: (i, 0)
    const = lambda i: (0, 0)
    return pl.pallas_call(
        functools.partial(_proj_ln_kernel, alpha=alpha),
        grid=(n // tm,),
        in_specs=[
            pl.BlockSpec((tm, k), row),
            pl.BlockSpec((tm, d), row),
            pl.BlockSpec((k, d), const),
            pl.BlockSpec((1, d), const),
            pl.BlockSpec((1, d), const),
        ],
        out_specs=[pl.BlockSpec((tm, d), row), pl.BlockSpec((tm, d), row)],
        out_shape=[jax.ShapeDtypeStruct((n, d), F32), jax.ShapeDtypeStruct((n, d), BF16)],
        compiler_params=_params("parallel"),
        name="out_proj_ln",
    )(y, x, w.astype(BF16), ln_g.reshape(1, d), ln_b.reshape(1, d))


def _qkv_kernel(x_ref, w_ref, o_ref, *, n_q_blocks, scale):
    j = pl.program_id(1)
    s = jnp.where(j < n_q_blocks, scale, 1.0).astype(F32)
    o_ref[...] = (_dot(x_ref[...], w_ref[...]) * s).astype(BF16)


def _qkv_proj(xb, w_in, d_model, scale, tm=1024, tn=1024):
    n, d = xb.shape
    nout = 3 * d_model
    w_qkv = w_in
    tm = min(tm, n)
    return pl.pallas_call(
        functools.partial(_qkv_kernel, n_q_blocks=d_model // tn, scale=scale),
        grid=(n // tm, nout // tn),
        in_specs=[pl.BlockSpec((tm, d), lambda i, j: (i, 0)), pl.BlockSpec((d, tn), lambda i, j: (0, j))],
        out_specs=pl.BlockSpec((tm, tn), lambda i, j: (i, j)),
        out_shape=jax.ShapeDtypeStruct((n, nout), BF16),
        compiler_params=_params("parallel", "arbitrary"),
        name="qkv_proj",
    )(xb, w_qkv)


def _fgate_kernel(x_ref, wf_ref, bf_ref, c_ref, ct_ref):
    z = _dot(x_ref[0], wf_ref[...]) + bf_ref[...]
    c = jnp.minimum(z, 0.0) - jnp.log1p(jnp.exp(-jnp.abs(z)))
    seq = c.shape[0]
    row = lax.broadcasted_iota(jnp.int32, c.shape, 0)
    k = 1
    while k < seq:
        c = c + jnp.where(row >= k, pltpu.roll(c, k, 0), 0.0)
        k *= 2
    c_ref[0] = c
    ct_ref[0] = c.T


def _fgate_cumsum(xb3, w_f, b_f):
    bsz, seq, d = xb3.shape
    wf = jnp.zeros((d, LANES), F32).at[:, :N_HEADS].set(w_f).astype(BF16)
    bf = jnp.zeros((1, LANES), F32).at[0, :N_HEADS].set(b_f)
    return pl.pallas_call(
        _fgate_kernel,
        grid=(bsz,),
        in_specs=[
            pl.BlockSpec((1, seq, d), lambda b: (b, 0, 0)),
            pl.BlockSpec((d, LANES), lambda b: (0, 0)),
            pl.BlockSpec((1, LANES), lambda b: (0, 0)),
        ],
        out_specs=[pl.BlockSpec((1, seq, LANES), lambda b: (b, 0, 0)),
                   pl.BlockSpec((1, LANES, seq), lambda b: (b, 0, 0))],
        out_shape=[jax.ShapeDtypeStruct((bsz, seq, LANES), F32),
                   jax.ShapeDtypeStruct((bsz, LANES, seq), F32)],
        compiler_params=_params("parallel"),
        name="forget_gate_cumsum",
    )(xb3, wf, bf)


def _flash_kernel(q_ref, k_ref, v_ref, c_ref, ct_ref, o_ref, *, tq):
    h = pl.program_id(1)
    seq = q_ref.shape[1]
    lane = lax.broadcasted_iota(jnp.int32, (seq, LANES), 1)
    cq_all = jnp.sum(jnp.where(lane == h, c_ref[0], 0.0), axis=-1, keepdims=True)
    rows = lax.broadcasted_iota(jnp.int32, (tq, tq), 0)
    cols = lax.broadcasted_iota(jnp.int32, (tq, tq), 1)
    causal = cols <= rows

    ck_all = ct_ref[0, pl.ds(h, 1), :]
    for qi in range(seq // tq):
        q = q_ref[0, qi * tq:(qi + 1) * tq, :]
        cq = cq_all[qi * tq:(qi + 1) * tq]
        m = l = acc = None
        for kb in range(qi + 1):
            k = k_ref[0, kb * tq:(kb + 1) * tq, :]
            v = v_ref[0, kb * tq:(kb + 1) * tq, :]
            s = lax.dot_general(q, k, (((1,), (1,)), ((), ())), preferred_element_type=F32)
            s = s + (cq - ck_all[:, kb * tq:(kb + 1) * tq])
            if kb == qi:
                s = jnp.where(causal, s, -jnp.inf)
            s_max = jnp.max(s, axis=-1, keepdims=True)
            if kb == 0:
                m = s_max
                p = jnp.exp(s - m)
                l = jnp.sum(p, axis=-1, keepdims=True)
                acc = _dot(p.astype(BF16), v)
            else:
                m_new = jnp.maximum(m, s_max)
                p = jnp.exp(s - m_new)
                a = jnp.exp(m - m_new)
                l = a * l + jnp.sum(p, axis=-1, keepdims=True)
                acc = a * acc + _dot(p.astype(BF16), v)
                m = m_new
        o_ref[0, qi * tq:(qi + 1) * tq, :] = (acc / l).astype(BF16)


def _flash_attention(qkv3, c, ct, d_model, tq=512):
    bsz, seq, _ = qkv3.shape
    dh = d_model // N_HEADS
    tq = min(tq, seq)
    return pl.pallas_call(
        functools.partial(_flash_kernel, tq=tq),
        grid=(bsz, N_HEADS),
        in_specs=[
            pl.BlockSpec((1, seq, dh), lambda b, h: (b, 0, h)),
            pl.BlockSpec((1, seq, dh), lambda b, h: (b, 0, N_HEADS + h)),
            pl.BlockSpec((1, seq, dh), lambda b, h: (b, 0, 2 * N_HEADS + h)),
            pl.BlockSpec((1, seq, LANES), lambda b, h: (b, 0, 0)),
            pl.BlockSpec((1, N_HEADS, seq), lambda b, h: (b, 0, 0)),
        ],
        out_specs=pl.BlockSpec((1, seq, dh), lambda b, h: (b, 0, h)),
        out_shape=jax.ShapeDtypeStruct((bsz, seq, d_model), BF16),
        compiler_params=_params("parallel", "arbitrary"),
        name="fox_flash_attention",
    )(qkv3, qkv3, qkv3, c, ct)


def _attention_layer(x, xb, w_in, b_f, w_o, ln_g, ln_b, alpha, bsz, seq):
    n, d = x.shape
    dh = d // N_HEADS
    qkv = _qkv_proj(xb, w_in.astype(BF16), d, dh ** -0.5)
    c, ct = _fgate_cumsum(xb.reshape(bsz, seq, d), w_in[:, 3 * d:], b_f)
    o = _flash_attention(qkv.reshape(bsz, seq, 3 * d), c, ct, d)
    return _proj_ln_layer(o.reshape(n, d), x, w_o, ln_g, ln_b, alpha)


def _rec_in_kernel(x_ref, w1_ref, w2_ref, o1_ref, o2_ref):
    x = x_ref[...]
    o1_ref[...] = _dot(x, w1_ref[...])
    o2_ref[...] = jax.nn.gelu(_dot(x, w2_ref[...]), approximate=True)


def _rec_in_proj(xb, w_in, tm=1024, tn=1024):
    n, d = xb.shape
    d_rnn = w_in.shape[1] // 2
    tm = min(tm, n)
    nj = d_rnn // tn
    w_in = w_in.astype(BF16)
    return pl.pallas_call(
        _rec_in_kernel,
        grid=(n // tm, nj),
        in_specs=[pl.BlockSpec((tm, d), lambda i, j: (i, 0)),
                  pl.BlockSpec((d, tn), lambda i, j: (0, j)),
                  pl.BlockSpec((d, tn), lambda i, j: (0, nj + j))],
        out_specs=[pl.BlockSpec((tm, tn), lambda i, j: (i, j)), pl.BlockSpec((tm, tn), lambda i, j: (i, j))],
        out_shape=[jax.ShapeDtypeStruct((n, d_rnn), F32), jax.ShapeDtypeStruct((n, d_rnn), F32)],
        compiler_params=_params("parallel", "arbitrary"),
        name="rglru_in_proj",
    )(xb, w_in, w_in)


def _softplus(z):
    return jnp.maximum(z, 0.0) + jnp.log1p(jnp.exp(-jnp.abs(z)))


def _rglru_kernel(xr_ref, halo_ref, gate_ref, cw_ref, cb_ref, wa_ref, ba_ref, wi_ref, bi_ref, lam_ref,
                  y_ref, xe_ref, a_ref, b_ref, hc_ref, *, tm, conv_width):
    i = pl.program_id(1)
    d_rnn = xr_ref.shape[-1]
    blk = d_rnn // RNN_HEADS
    for h in range(RNN_HEADS):
        xe_ref[h, SUBLANES:, :] = xr_ref[0, :, h * blk:(h + 1) * blk]

    @pl.when(i == 0)
    def _():
        xe_ref[:, :SUBLANES, :] = jnp.zeros((RNN_HEADS, SUBLANES, blk), F32)
        hc_ref[...] = jnp.zeros_like(hc_ref)

    @pl.when(i > 0)
    def _():
        for h in range(RNN_HEADS):
            xe_ref[h, :SUBLANES, :] = halo_ref[0, :, h * blk:(h + 1) * blk]

    first_row = jnp.logical_and(i == 0, lax.broadcasted_iota(jnp.int32, (tm, blk), 0) == 0)
    for h in range(RNN_HEADS):
        cols = slice(h * blk, (h + 1) * blk)
        xb = cb_ref[:, cols]
        for k in range(conv_width):
            off = SUBLANES - (conv_width - 1) + k
            xb = xb + cw_ref[k:k + 1, cols] * xe_ref[h, pl.ds(off, tm, stride=1), :]
        xb16 = xb.astype(BF16)
        r = jax.nn.sigmoid(_dot(xb16, wa_ref[h]) + ba_ref[:, cols])
        ig = jax.nn.sigmoid(_dot(xb16, wi_ref[h]) + bi_ref[:, cols])
        log_a = (-LRU_C) * r * _softplus(-lam_ref[:, cols])
        a = jnp.exp(log_a)
        mult = jnp.sqrt(-jnp.tanh(log_a) * (a * a + 1.0))
        mult = jnp.where(first_row, 1.0, mult)
        a_ref[:, cols] = a
        b_ref[:, cols] = mult * (ig * xb)

    row = lax.broadcasted_iota(jnp.int32, (SUBLANES, d_rnn), 0)

    def slab(s, hprev):
        start = pl.multiple_of(s * SUBLANES, SUBLANES)
        a = a_ref[pl.ds(start, SUBLANES), :]
        b = b_ref[pl.ds(start, SUBLANES), :]
        for dist in (1, 2, 4):
            keep = row >= dist
            b = jnp.where(keep, a * pltpu.roll(b, dist, 0) + b, b)
            a = jnp.where(keep, a * pltpu.roll(a, dist, 0), a)
        hcur = a * hprev + b
        b_ref[pl.ds(start, SUBLANES), :] = hcur
        return jnp.broadcast_to(hcur[SUBLANES - 1:SUBLANES, :], (SUBLANES, d_rnn))

    hc_ref[...] = lax.fori_loop(0, tm // SUBLANES, slab, hc_ref[...])
    y_ref[0] = (b_ref[...] * gate_ref[0]).astype(BF16)


def _rglru_core(xr3, gate3, conv_w, conv_b, w_a, b_a, w_i, b_i, lam, tm=256):
    bsz, seq, d = xr3.shape
    tm = min(tm, seq)
    hblk = tm // SUBLANES
    blk = d // RNN_HEADS
    assert blk == LANES, "the conv scratch keeps one 128-lane slab per recurrent head"
    row = lambda b, i: (b, i, 0)
    const2 = lambda b, i: (0, 0)
    const3 = lambda b, i: (0, 0, 0)
    vec = lambda v: v.reshape(1, d)
    return pl.pallas_call(
        functools.partial(_rglru_kernel, tm=tm, conv_width=conv_w.shape[0]),
        grid=(bsz, seq // tm),
        in_specs=[
            pl.BlockSpec((1, tm, d), row),
            pl.BlockSpec((1, SUBLANES, d), lambda b, i: (b, jnp.maximum(i * hblk - 1, 0), 0)),
            pl.BlockSpec((1, tm, d), row),
            pl.BlockSpec((conv_w.shape[0], d), const2),
            pl.BlockSpec((1, d), const2),
            pl.BlockSpec((RNN_HEADS, blk, blk), const3),
            pl.BlockSpec((1, d), const2),
            pl.BlockSpec((RNN_HEADS, blk, blk), const3),
            pl.BlockSpec((1, d), const2),
            pl.BlockSpec((1, d), const2),
        ],
        out_specs=pl.BlockSpec((1, tm, d), row),
        out_shape=jax.ShapeDtypeStruct((bsz, seq, d), BF16),
        scratch_shapes=[
            pltpu.VMEM((RNN_HEADS, tm + SUBLANES, blk), F32),
            pltpu.VMEM((tm, d), F32),
            pltpu.VMEM((tm, d), F32),
            pltpu.VMEM((SUBLANES, d), F32),
        ],
        compiler_params=_params("parallel", "arbitrary"),
        name="rglru_scan",
    )(xr3, xr3, gate3, conv_w, vec(conv_b), w_a.astype(BF16), vec(b_a), w_i.astype(BF16), vec(b_i), vec(lam))


def _rglru_layer(x, xb, w_in, conv_w, conv_b, w_a, b_a, w_i, b_i, lam, w_o, ln_g, ln_b, alpha, bsz, seq):
    n, d = x.shape
    xr, gate = _rec_in_proj(xb, w_in)
    d_rnn = xr.shape[1]
    y = _rglru_core(xr.reshape(bsz, seq, d_rnn), gate.reshape(bsz, seq, d_rnn),
                    conv_w, conv_b, w_a, b_a, w_i, b_i, lam)
    return _proj_ln_layer(y.reshape(n, d_rnn), x, w_o, ln_g, ln_b, alpha)


def kernel(x, pool_w, pool_scale, attn_w_in, attn_b_f, attn_w_o, rec_w_in, rec_conv_w, rec_conv_b, rec_w_a, rec_b_a, rec_w_i, rec_b_i, rec_lam, rec_w_o, ln_g, ln_b, ffn_w_up, ffn_conv_w, ffn_conv_b, ffn_w_down):
    bsz, seq, d = x.shape
    depth = ln_g.shape[0]
    alpha = (2.0 * depth) ** 0.25
    n = bsz * seq
    w_up_b = ffn_w_up.astype(BF16)
    w_down_b = ffn_w_down.astype(BF16)
    conv_b3 = ffn_conv_b.reshape(depth, 1, ffn_conv_b.shape[-1])
    ln_g3 = ln_g.reshape(2 * depth, 1, d)
    ln_b3 = ln_b.reshape(2 * depth, 1, d)
    xf = x
    xb = None
    for layer in range(depth):
        kind = layer % N_MIXERS
        j = layer // N_MIXERS
        if kind == 0:
            xf3, xb3 = _pool_layer(xf.reshape(bsz, seq, d), pool_w[j], pool_scale[j],
                                   ln_g[layer, 0], ln_b[layer, 0], alpha)
            xf, xb = xf3.reshape(n, d), xb3.reshape(n, d)
        elif kind == 1:
            xf, xb = _attention_layer(xf, xb, attn_w_in[j], attn_b_f[j], attn_w_o[j],
                                      ln_g[layer, 0], ln_b[layer, 0], alpha, bsz, seq)
        else:
            xf, xb = _rglru_layer(xf, xb, rec_w_in[j], rec_conv_w[j], rec_conv_b[j], rec_w_a[j], rec_b_a[j],
                                  rec_w_i[j], rec_b_i[j], rec_lam[j], rec_w_o[j],
                                  ln_g[layer, 0], ln_b[layer, 0], alpha, bsz, seq)
        xf, xb = _ffn_layer(xf, xb, w_up_b, ffn_conv_w, conv_b3, w_down_b, ln_g3, ln_b3, layer, alpha, seq)
    return xf.reshape(bsz, seq, d)
```
